```python
import math
import jax, jax.numpy as jnp
from jax import lax
import numpy as np

D_MODEL = 1024
BATCH = 2
SEQ = 16384
DEPTH = 4

N_MIXERS = 2
N_CONV_LAYERS = (DEPTH + 1) // 2
N_ATTN_LAYERS = DEPTH // 2
N_HEADS = 16
HEAD_DIM = D_MODEL // N_HEADS
MOBA_BLOCK = 256
MOBA_TOP_K = 3
Q_CHUNK = 64
CONV_WIDTH = 31
D_FF = 4 * D_MODEL
N_MOD = 6
EPS = 1e-6

kernel_name = "hybrid_conv_moba_adaln_trunk"


def rms_norm(x, g):
    xf = x.astype(jnp.float32)
    y = xf * lax.rsqrt(jnp.mean(xf * xf, axis=-1, keepdims=True) + EPS)
    return (y * g.astype(jnp.float32)).astype(x.dtype)


def layer_norm(x, g, b):
    xf = x.astype(jnp.float32)
    mu = jnp.mean(xf, axis=-1, keepdims=True)
    xc = xf - mu
    var = jnp.mean(xc * xc, axis=-1, keepdims=True)
    y = xc * lax.rsqrt(var + EPS) * g.astype(jnp.float32) + b.astype(jnp.float32)
    return y.astype(x.dtype)


def alibi_slopes(n_heads):
    return jnp.asarray(2.0 ** (-8.0 * np.arange(1, n_heads + 1) / n_heads), dtype=jnp.float32)


def conformer_conv(h, w_in, b_in, dw, dw_b, ln_g, ln_b, w_out, b_out):
    d = h.shape[-1]
    a, g = jnp.split(h @ w_in + b_in, 2, axis=-1)
    u = a * jax.nn.sigmoid(g)
    u = lax.conv_general_dilated(
        u, dw[:, None, :].astype(u.dtype), window_strides=(1,),
        padding=((CONV_WIDTH - 1, 0),),
        dimension_numbers=('NWC', 'WIO', 'NWC'),
        feature_group_count=d) + dw_b
    u = jax.nn.silu(layer_norm(u, ln_g, ln_b))
    return u @ w_out + b_out


def moba_attention(h, w_qkv, q_norm_g, k_norm_g, w_o):
    B, S, D = h.shape
    H, DH = N_HEADS, HEAD_DIM
    q, k, v = jnp.split(h @ w_qkv, 3, axis=-1)
    to_heads = lambda z: z.reshape(B, S, H, DH).transpose(0, 2, 1, 3)
    q = rms_norm(to_heads(q), q_norm_g)
    k = rms_norm(to_heads(k), k_norm_g)
    v = to_heads(v)

    nb = -(-S // MOBA_BLOCK)
    s_pad = nb * MOBA_BLOCK
    pad = ((0, 0), (0, 0), (0, s_pad - S), (0, 0))
    q, k, v = jnp.pad(q, pad), jnp.pad(k, pad), jnp.pad(v, pad)
    k_blocks = k.reshape(B, H, nb, MOBA_BLOCK, DH)
    v_blocks = v.reshape(B, H, nb, MOBA_BLOCK, DH)
    k_mean = jnp.mean(k_blocks.astype(jnp.float32), axis=3)

    topk = min(MOBA_TOP_K, nb)
    slopes = alibi_slopes(H)
    scale = 1.0 / math.sqrt(DH)
    n_chunks = s_pad // Q_CHUNK
    q_chunks = q.reshape(B, H, n_chunks, Q_CHUNK, DH).transpose(2, 0, 1, 3, 4)
    bi = jnp.arange(B)[:, None, None, None]
    hi = jnp.arange(H)[None, :, None, None]
    blk_ids = jnp.arange(nb)
    in_block = jnp.arange(MOBA_BLOCK)

    def chunk_fn(args):
        qc, ci = args
        t = ci * Q_CHUNK + jnp.arange(Q_CHUNK)
        blk = (ci * Q_CHUNK) // MOBA_BLOCK
        gate = jnp.einsum('bhqd,bhnd->bhqn', qc.astype(jnp.float32), k_mean)
        gate = jnp.where(blk_ids < blk, gate, -jnp.inf)
        gate_val, sel = lax.top_k(gate, topk)
        sel_ok = jnp.isfinite(gate_val)
        k_sel = k_blocks[bi, hi, sel]
        v_sel = v_blocks[bi, hi, sel]
        s_sel = jnp.einsum('bhqd,bhqcsd->bhqcs', qc, k_sel).astype(jnp.float32) * scale
        pos_sel = sel[..., None] * MOBA_BLOCK + in_block
        dist_sel = t[None, None, :, None, None] - pos_sel
        s_sel = s_sel - slopes[None, :, None, None, None] * dist_sel
        s_sel = jnp.where(sel_ok[..., None], s_sel, -jnp.inf)
        k_own = lax.dynamic_slice_in_dim(k, blk * MOBA_BLOCK, MOBA_BLOCK, axis=2)
        v_own = lax.dynamic_slice_in_dim(v, blk * MOBA_BLOCK, MOBA_BLOCK, axis=2)
        s_own = jnp.einsum('bhqd,bhsd->bhqs', qc, k_own).astype(jnp.float32) * scale
        dist_own = t[:, None] - (blk * MOBA_BLOCK + in_block)[None, :]
        s_own = jnp.where(dist_own >= 0, s_own - slopes[None, :, None, None] * dist_own, -jnp.inf)
        scores = jnp.concatenate([s_sel.reshape(B, H, Q_CHUNK, topk * MOBA_BLOCK), s_own], axis=-1)
        p = jax.nn.softmax(scores, axis=-1)
        p_sel = p[..., :topk * MOBA_BLOCK].reshape(B, H, Q_CHUNK, topk, MOBA_BLOCK).astype(v.dtype)
        p_own = p[..., topk * MOBA_BLOCK:].astype(v.dtype)
        return (jnp.einsum('bhqcs,bhqcsd->bhqd', p_sel, v_sel)
                + jnp.einsum('bhqs,bhsd->bhqd', p_own, v_own))

    out = lax.map(chunk_fn, (q_chunks, jnp.arange(n_chunks)))
    out = out.transpose(1, 2, 0, 3, 4).reshape(B, H, s_pad, DH)[:, :, :S]
    out = out.transpose(0, 2, 1, 3).reshape(B, S, D)
    return out @ w_o


def setup_inputs(seed: int = 0) -> dict:
    key = jax.random.key(seed)
    ks = iter(jax.random.split(key, 32))
    nrm = lambda shape, s: jax.random.normal(next(ks), shape, jnp.float32) * s
    D = D_MODEL
    return {
        "x": nrm((BATCH, SEQ, D), 1.0),
        "c": nrm((BATCH, D), 1.0),
        "ada_w": nrm((DEPTH, D, N_MOD * D), 0.5 * D ** -0.5),
        "ada_b": nrm((DEPTH, N_MOD * D), 0.02),
        "mix_norm_g": 1.0 + nrm((DEPTH, D), 0.1),
        "mlp_norm_g": 1.0 + nrm((DEPTH, D), 0.1),
        "conv_w_in": nrm((N_CONV_LAYERS, D, 2 * D), D ** -0.5),
        "conv_b_in": nrm((N_CONV_LAYERS, 2 * D), 0.02),
        "conv_dw": nrm((N_CONV_LAYERS, CONV_WIDTH, D), CONV_WIDTH ** -0.5),
        "conv_dw_b": nrm((N_CONV_LAYERS, D), 0.02),
        "conv_ln_g": 1.0 + nrm((N_CONV_LAYERS, D), 0.1),
        "conv_ln_b": nrm((N_CONV_LAYERS, D), 0.02),
        "conv_w_out": nrm((N_CONV_LAYERS, D, D), D ** -0.5),
        "conv_b_out": nrm((N_CONV_LAYERS, D), 0.02),
        "attn_w_qkv": nrm((N_ATTN_LAYERS, D, 3 * D), D ** -0.5),
        "attn_q_norm_g": 1.0 + nrm((N_ATTN_LAYERS, HEAD_DIM), 0.1),
        "attn_k_norm_g": 1.0 + nrm((N_ATTN_LAYERS, HEAD_DIM), 0.1),
        "attn_w_o": nrm((N_ATTN_LAYERS, D, D), D ** -0.5),
        "mlp_w1": nrm((DEPTH, D, D_FF), D ** -0.5),
        "mlp_w2": nrm((DEPTH, D_FF, D), D_FF ** -0.5),
    }


def reference(x, c, ada_w, ada_b, mix_norm_g, mlp_norm_g,
              conv_w_in, conv_b_in, conv_dw, conv_dw_b, conv_ln_g, conv_ln_b, conv_w_out, conv_b_out,
              attn_w_qkv, attn_q_norm_g, attn_k_norm_g, attn_w_o,
              mlp_w1, mlp_w2):
    cond = jax.nn.silu(c)
    for i in range(DEPTH):
        mod = cond @ ada_w[i] + ada_b[i]
        sh1, sc1, g1, sh2, sc2, g2 = [m[:, None, :] for m in jnp.split(mod, N_MOD, axis=-1)]
        h = rms_norm(x, mix_norm_g[i]) * (1.0 + sc1) + sh1
        j = i // N_MIXERS
        if i % N_MIXERS == 0:
            y = conformer_conv(h, conv_w_in[j], conv_b_in[j], conv_dw[j], conv_dw_b[j],
                               conv_ln_g[j], conv_ln_b[j], conv_w_out[j], conv_b_out[j])
        else:
            y = moba_attention(h, attn_w_qkv[j], attn_q_norm_g[j], attn_k_norm_g[j], attn_w_o[j])
        x = x + g1 * y
        h = rms_norm(x, mlp_norm_g[i]) * (1.0 + sc2) + sh2
        x = x + g2 * (jnp.square(jax.nn.relu(h @ mlp_w1[i])) @ mlp_w2[i])
    return x
```

```python
import functools

import numpy as np
import jax
import jax.numpy as jnp
from jax import lax
from jax.experimental import pallas as pl
from jax.experimental.pallas import tpu as pltpu

D_MODEL = 1024
N_HEADS = 16
HEAD_DIM = D_MODEL // N_HEADS
MOBA_BLOCK = 256
MOBA_TOP_K = 3
CONV_WIDTH = 31
D_FF = 4 * D_MODEL
N_MOD = 6
EPS = 1e-6

LANES = 128
HEADS_PER_TILE = LANES // HEAD_DIM
CONV_HALO = 32
CONV_ROWS = 32
ROW_TILE = 512
MASKED = -1e30
VMEM_LIMIT = 56 * 1024 * 1024

F32 = jnp.float32
BF16 = jnp.bfloat16
_NT = (((1,), (1,)), ((), ()))


def _resident(shape, index_map):
    return pl.BlockSpec(shape, index_map, pipeline_mode=pl.Buffered(1))


def _params(*semantics):
    return pltpu.CompilerParams(dimension_semantics=semantics, vmem_limit_bytes=VMEM_LIMIT)


def _norm_mod(x, g, scale, shift):
    ms = jnp.mean(x * x, axis=-1, keepdims=True)
    return (x * lax.rsqrt(ms + EPS) * g) * (1.0 + scale) + shift


def _mod_kernel(c_ref, w_ref, b_ref, o_ref):
    c = c_ref[...]
    cond = c * jax.nn.sigmoid(c)
    o_ref[0, 0] = jnp.dot(cond, w_ref[0], precision=lax.Precision.HIGHEST,
                          preferred_element_type=F32) + b_ref[0]


def _modulation(c, ada_w, ada_b):
    depth = ada_w.shape[0]
    batch = c.shape[0]
    rows = 8 * pl.cdiv(batch, 8)
    c_pad = jnp.zeros((rows, D_MODEL), F32).at[:batch].set(c)
    out = pl.pallas_call(
        _mod_kernel,
        grid=(depth, N_MOD),
        in_specs=[
            pl.BlockSpec((rows, D_MODEL), lambda l, k: (0, 0)),
            pl.BlockSpec((1, D_MODEL, D_MODEL), lambda l, k: (l, 0, k)),
            pl.BlockSpec((1, 1, D_MODEL), lambda l, k: (l, 0, k)),
        ],
        out_specs=pl.BlockSpec((1, 1, rows, D_MODEL), lambda l, k: (l, k, 0, 0)),
        out_shape=jax.ShapeDtypeStruct((depth, N_MOD, rows, D_MODEL), F32),
        compiler_params=_params("arbitrary", "arbitrary"),
        name="adaln_modulation",
    )(c_pad, ada_w, ada_b.reshape(depth, 1, N_MOD * D_MODEL))
    mods = out[:, :, :batch].transpose(0, 2, 1, 3)
    return jnp.pad(mods, ((0, 0), (0, 0), (0, 8 - N_MOD), (0, 0)))


def _proj_kernel(x_ref, mod_ref, g_ref, w_ref, *rest, glu, has_bias):
    o_ref = rest[-1]
    m = mod_ref[0]
    h = _norm_mod(x_ref[...], g_ref[...], m[1:2], m[0:1])
    y = jnp.dot(h.astype(BF16), w_ref[...], preferred_element_type=F32)
    if has_bias:
        y = y + rest[0][...]
    if glu:
        half = y.shape[-1] // 2
        y = y[:, :half] * jax.nn.sigmoid(y[:, half:])
        for t in range(half // LANES):
            o_ref[t] = y[:, t * LANES:(t + 1) * LANES]
    else:
        o_ref[...] = y


def _projection(x, mod, g, w, b, seq, glu):
    tokens = x.shape[0]
    n_out = w.shape[1]
    n_res = n_out // 2 if glu else n_out
    tiles_per_seq = seq // ROW_TILE
    has_bias = b is not None
    in_specs = [
        pl.BlockSpec((ROW_TILE, D_MODEL), lambda t: (t, 0)),
        pl.BlockSpec((1, 8, D_MODEL), lambda t: (t // tiles_per_seq, 0, 0)),
        _resident((1, D_MODEL), lambda t: (0, 0)),
        _resident((D_MODEL, n_out), lambda t: (0, 0)),
    ]
    args = [x, mod, g.reshape(1, D_MODEL), w.astype(BF16)]
    if has_bias:
        in_specs.append(_resident((1, n_out), lambda t: (0, 0)))
        args.append(b.reshape(1, n_out))
    if glu:
        out_spec = pl.BlockSpec((n_res // LANES, ROW_TILE, LANES), lambda t: (0, t, 0))
        out_shape = jax.ShapeDtypeStruct((n_res // LANES, tokens, LANES), F32)
    else:
        out_spec = pl.BlockSpec((ROW_TILE, n_res), lambda t: (t, 0))
        out_shape = jax.ShapeDtypeStruct((tokens, n_res), F32)
    return pl.pallas_call(
        functools.partial(_proj_kernel, glu=glu, has_bias=has_bias),
        grid=(tokens // ROW_TILE,),
        in_specs=in_specs,
        out_specs=out_spec,
        out_shape=out_shape,
        compiler_params=_params("arbitrary"),
        name="glu_projection" if glu else "qkv_projection",
    )(*args)


def _conv_kernel(u_ref, dw_ref, dwb_ref, lng_ref, lnb_ref, o_ref, ext_ref, *, tiles_per_seq):
    t = pl.program_id(0)
    n_lt, rows, _ = u_ref.shape

    @pl.when(t % tiles_per_seq == 0)
    def _():
        ext_ref[:, 0:CONV_HALO, :] = jnp.zeros((n_lt, CONV_HALO, LANES), F32)

    @pl.when(t % tiles_per_seq != 0)
    def _():
        ext_ref[:, 0:CONV_HALO, :] = ext_ref[:, rows:rows + CONV_HALO, :]

    ext_ref[:, CONV_HALO:CONV_HALO + rows, :] = u_ref[...]

    first_tap = CONV_HALO - (CONV_WIDTH - 1)

    def chunk(c, carry):
        r0 = pl.multiple_of(c * CONV_ROWS, CONV_ROWS)
        accs = []
        for lt in range(n_lt):
            cols = slice(lt * LANES, (lt + 1) * LANES)
            acc = jnp.broadcast_to(dwb_ref[:, cols], (CONV_ROWS, LANES))
            for k in range(CONV_WIDTH):
                acc = acc + dw_ref[k:k + 1, cols] * ext_ref[lt, pl.ds(r0 + first_tap + k, CONV_ROWS), :]
            accs.append(acc)
        total = accs[0]
        for a in accs[1:]:
            total = total + a
        mu = jnp.sum(total, axis=-1, keepdims=True) * (1.0 / D_MODEL)
        cen = [a - mu for a in accs]
        sq = cen[0] * cen[0]
        for a in cen[1:]:
            sq = sq + a * a
        inv = lax.rsqrt(jnp.sum(sq, axis=-1, keepdims=True) * (1.0 / D_MODEL) + EPS)
        for lt in range(n_lt):
            cols = slice(lt * LANES, (lt + 1) * LANES)
            y = cen[lt] * inv * lng_ref[:, cols] + lnb_ref[:, cols]
            o_ref[pl.ds(r0, CONV_ROWS), cols] = (y * jax.nn.sigmoid(y)).astype(BF16)
        return carry

    lax.fori_loop(0, rows // CONV_ROWS, chunk, 0)


def _conv_mixer(u, dw, dw_b, ln_g, ln_b, seq):
    n_lt, tokens, _ = u.shape
    tiles_per_seq = seq // ROW_TILE
    taps = 8 * pl.cdiv(CONV_WIDTH, 8)
    dw_pad = jnp.zeros((taps, D_MODEL), F32).at[:CONV_WIDTH].set(dw)
    row = lambda v: v.reshape(1, D_MODEL)
    return pl.pallas_call(
        functools.partial(_conv_kernel, tiles_per_seq=tiles_per_seq),
        grid=(tokens // ROW_TILE,),
        in_specs=[
            pl.BlockSpec((n_lt, ROW_TILE, LANES), lambda t: (0, t, 0)),
            _resident((taps, D_MODEL), lambda t: (0, 0)),
            _resident((1, D_MODEL), lambda t: (0, 0)),
            _resident((1, D_MODEL), lambda t: (0, 0)),
            _resident((1, D_MODEL), lambda t: (0, 0)),
        ],
        out_specs=pl.BlockSpec((ROW_TILE, D_MODEL), lambda t: (t, 0)),
        out_shape=jax.ShapeDtypeStruct((tokens, D_MODEL), BF16),
        scratch_shapes=[pltpu.VMEM((n_lt, ROW_TILE + CONV_HALO, LANES), F32)],
        compiler_params=_params("arbitrary"),
        name="conv_mixer",
    )(u, dw_pad, row(dw_b), row(ln_g), row(ln_b))


def _attn_kernel(slope_ref, q_ref, k_ref, v_ref, gq_ref, gk_ref, o_ref, kb_ref, vb_ref, km_ref, *, n_blocks):
    tile = pl.program_id(1)
    j = pl.program_id(2)
    blk = MOBA_BLOCK

    lane = lax.broadcasted_iota(jnp.int32, (blk, LANES), 1)
    low = lane < HEAD_DIM
    lane_blk = lane & (HEAD_DIM - 1)

    def head_rms(x, g):
        x2 = x * x
        s0 = jnp.sum(jnp.where(low, x2, 0.0), axis=-1, keepdims=True)
        s1 = jnp.sum(jnp.where(low, 0.0, x2), axis=-1, keepdims=True)
        r = jnp.where(low, lax.rsqrt(s0 * (1.0 / HEAD_DIM) + EPS), lax.rsqrt(s1 * (1.0 / HEAD_DIM) + EPS))
        return x * r * g

    @pl.when(j == 0)
    def _():
        km_ref[...] = jnp.zeros(km_ref.shape, F32)

        def prep(n, carry):
            r0 = pl.multiple_of(n * blk, blk)
            kn = head_rms(k_ref[0, pl.ds(r0, blk), :], gk_ref[...])
            kb_ref[pl.ds(r0, blk), :] = kn.astype(BF16)
            km_ref[pl.ds(n, 1), :] = jnp.mean(kn, axis=0, keepdims=True)
            vb_ref[pl.ds(r0, blk), :] = v_ref[0, pl.ds(r0, blk), :].astype(BF16)
            return carry

        lax.fori_loop(0, n_blocks, prep, 0)

    q = head_rms(q_ref[0], gq_ref[...]) * (HEAD_DIM ** -0.5)
    q_bf = q.astype(BF16)
    km = km_ref[...]
    km2 = jnp.concatenate([km, km], axis=0)

    qi = lax.broadcasted_iota(jnp.int32, (blk, blk), 0)
    ki = lax.broadcasted_iota(jnp.int32, (blk, blk), 1)
    rel = (qi - ki).astype(F32)
    causal = qi >= ki

    outs = []
    for h in range(HEADS_PER_TILE):
        mine = low if h == 0 else jnp.logical_not(low)
        slope = slope_ref[tile * HEADS_PER_TILE + h]

        gate = lax.dot_general(jnp.where(mine, q, 0.0), km2, _NT,
                               precision=lax.Precision.HIGHEST, preferred_element_type=F32)
        gate = jnp.where(lane_blk < j, gate, -jnp.inf)
        lane_f = lane_blk.astype(F32)
        sel = jnp.where(lane_blk == j, 1.0, 0.0)
        for _ in range(MOBA_TOP_K):
            best = jnp.max(gate, axis=-1, keepdims=True)
            first = jnp.min(jnp.where(gate == best, lane_f, float(HEAD_DIM)), axis=-1, keepdims=True)
            first = jnp.where(best > -jnp.inf, first, -1.0)
            pick = lane_f == first
            sel = jnp.where(pick, 1.0, sel)
            gate = jnp.where(pick, -jnp.inf, gate)

        add = jnp.where(sel > 0.0, 0.0, MASKED) + (slope * float(blk)) * (lane_blk - j).astype(F32)
        a1 = add.astype(BF16)
        r1 = add - a1.astype(F32)
        a2 = r1.astype(BF16)
        a3 = (r1 - a2.astype(F32)).astype(BF16)
        q_aug = jnp.concatenate([jnp.where(mine, q_bf, a1), jnp.where(low, a2, a3)], axis=1)

        local_bias = (-slope) * rel

        def scores(n):
            r0 = pl.multiple_of(n * blk, blk)
            one_hot = jnp.where(lane_blk == n, 1.0, 0.0).astype(BF16)
            k_aug = jnp.concatenate([jnp.where(mine, kb_ref[pl.ds(r0, blk), :], one_hot), one_hot], axis=1)
            return lax.dot_general(q_aug, k_aug, _NT, preferred_element_type=F32) + local_bias

        def pv(p, n):
            r0 = pl.multiple_of(n * blk, blk)
            return jnp.dot(p.astype(BF16), vb_ref[pl.ds(r0, blk), :], preferred_element_type=F32)

        s = jnp.where(causal, scores(j), -jnp.inf)
        m = jnp.max(s, axis=-1, keepdims=True)
        p = jnp.exp(s - m)
        l = jnp.sum(p, axis=-1, keepdims=True)
        acc = pv(p, j)

        def step(n, carry):
            m, l, acc = carry
            s = scores(n)
            m_new = jnp.maximum(m, jnp.max(s, axis=-1, keepdims=True))
            alpha = jnp.exp(m - m_new)
            p = jnp.exp(s - m_new)
            l = alpha * l + jnp.sum(p, axis=-1, keepdims=True)
            acc = alpha * acc + pv(p, n)
            return m_new, l, acc

        m, l, acc = lax.fori_loop(0, j, step, (m, l, acc))
        outs.append(acc / l)

    o_ref[0] = jnp.where(low, outs[0], outs[1]).astype(BF16)


def _attention(qkv, q_g, k_g, batch, seq):
    assert seq % MOBA_BLOCK == 0 and HEADS_PER_TILE == 2
    n_blocks = seq // MOBA_BLOCK
    assert n_blocks <= HEAD_DIM, "selection layout holds one key block per lane of a head"
    n_tiles = D_MODEL // LANES
    slopes = jnp.asarray(2.0 ** (-8.0 * np.arange(1, N_HEADS + 1) / N_HEADS), dtype=F32)
    qkv3 = qkv.reshape(batch, seq, 3 * D_MODEL)
    tile2 = lambda g: jnp.tile(g.reshape(1, HEAD_DIM), (1, HEADS_PER_TILE))
    out = pl.pallas_call(
        functools.partial(_attn_kernel, n_blocks=n_blocks),
        grid_spec=pltpu.PrefetchScalarGridSpec(
            num_scalar_prefetch=1,
            grid=(batch, n_tiles, n_blocks),
            in_specs=[
                pl.BlockSpec((1, MOBA_BLOCK, LANES), lambda b, t, j, s: (b, j, t)),
                pl.BlockSpec((1, seq, LANES), lambda b, t, j, s: (b, 0, n_tiles + t),
                             pipeline_mode=pl.Buffered(1)),
                pl.BlockSpec((1, seq, LANES), lambda b, t, j, s: (b, 0, 2 * n_tiles + t),
                             pipeline_mode=pl.Buffered(1)),
                pl.BlockSpec((1, LANES), lambda b, t, j, s: (0, 0)),
                pl.BlockSpec((1, LANES), lambda b, t, j, s: (0, 0)),
            ],
            out_specs=pl.BlockSpec((1, MOBA_BLOCK, LANES), lambda b, t, j, s: (b, j, t)),
            scratch_shapes=[
                pltpu.VMEM((seq, LANES), BF16),
                pltpu.VMEM((seq, LANES), BF16),
                pltpu.VMEM((HEAD_DIM, LANES), F32),
            ],
        ),
        out_shape=jax.ShapeDtypeStruct((batch, seq, D_MODEL), BF16),
        compiler_params=_params("arbitrary", "arbitrary", "arbitrary"),
        name="moba_attention",
    )(slopes, qkv3, qkv3, qkv3, tile2(q_g), tile2(k_g))
    return out.reshape(batch * seq, D_MODEL)


def _mlp_kernel(x_ref, y_ref, mod_ref, g_ref, wo_ref, bo_ref, w1_ref, w2_ref, o_ref):
    m = mod_ref[0]
    mixed = jnp.dot(y_ref[...], wo_ref[...], preferred_element_type=F32) + bo_ref[...]
    x = x_ref[...] + m[2:3] * mixed
    h = _norm_mod(x, g_ref[...], m[4:5], m[3:4]).astype(BF16)
    acc = jnp.zeros(x.shape, F32)
    for c in range(D_FF // D_MODEL):
        cols = slice(c * D_MODEL, (c + 1) * D_MODEL)
        a = jnp.maximum(jnp.dot(h, w1_ref[:, cols], preferred_element_type=F32), 0.0)
        acc = acc + jnp.dot((a * a).astype(BF16), w2_ref[cols, :], preferred_element_type=F32)
    o_ref[...] = x + m[5:6] * acc


def _mix_out_mlp(x, y, mod, g, w_o, b_o, w1, w2, seq):
    tokens = x.shape[0]
    tiles_per_seq = seq // ROW_TILE
    return pl.pallas_call(
        _mlp_kernel,
        grid=(tokens // ROW_TILE,),
        in_specs=[
            pl.BlockSpec((ROW_TILE, D_MODEL), lambda t: (t, 0)),
            pl.BlockSpec((ROW_TILE, D_MODEL), lambda t: (t, 0)),
            pl.BlockSpec((1, 8, D_MODEL), lambda t: (t // tiles_per_seq, 0, 0)),
            _resident((1, D_MODEL), lambda t: (0, 0)),
            _resident((D_MODEL, D_MODEL), lambda t: (0, 0)),
            _resident((1, D_MODEL), lambda t: (0, 0)),
            _resident((D_MODEL, D_FF), lambda t: (0, 0)),
            _resident((D_FF, D_MODEL), lambda t: (0, 0)),
        ],
        out_specs=pl.BlockSpec((ROW_TILE, D_MODEL), lambda t: (t, 0)),
        out_shape=jax.ShapeDtypeStruct((tokens, D_MODEL), F32),
        compiler_params=_params("arbitrary"),
        name="mix_out_mlp",
    )(x, y, mod, g.reshape(1, D_MODEL), w_o.astype(BF16), b_o.reshape(1, D_MODEL),
      w1.astype(BF16), w2.astype(BF16))


def kernel(x, c, ada_w, ada_b, mix_norm_g, mlp_norm_g, conv_w_in, conv_b_in, conv_dw, conv_dw_b, conv_ln_g, conv_ln_b, conv_w_out, conv_b_out, attn_w_qkv, attn_q_norm_g, attn_k_norm_g, attn_w_o, mlp_w1, mlp_w2):
    batch, seq, d = x.shape
    assert d == D_MODEL and seq % ROW_TILE == 0
    depth = ada_w.shape[0]
    mods = _modulation(c, ada_w, ada_b)
    xf = x.reshape(batch * seq, D_MODEL)
    for i in range(depth):
        j = i // 2
        if i % 2 == 0:
            u = _projection(xf, mods[i], mix_norm_g[i], conv_w_in[j], conv_b_in[j], seq, glu=True)
            y = _conv_mixer(u, conv_dw[j], conv_dw_b[j], conv_ln_g[j], conv_ln_b[j], seq)
            w_o, b_o = conv_w_out[j], conv_b_out[j]
        else:
            qkv = _projection(xf, mods[i], mix_norm_g[i], attn_w_qkv[j], None, seq, glu=False)
            y = _attention(qkv, attn_q_norm_g[j], attn_k_norm_g[j], batch, seq)
            w_o, b_o = attn_w_o[j], jnp.zeros((D_MODEL,), F32)
        xf = _mix_out_mlp(xf, y, mods[i], mlp_norm_g[i], w_o, b_o, mlp_w1[i], mlp_w2[i], seq)
    return xf.reshape(batch, seq, D_MODEL)
```

```python
import functools

import numpy as np
import jax
import jax.numpy as jnp
from jax import lax
from jax.experimental import pallas as pl
from jax.experimental.pallas import tpu as pltpu

D_MODEL = 1024
N_HEADS = 16
HEAD_DIM = D_MODEL // N_HEADS
MOBA_BLOCK = 256
MOBA_TOP_K = 3
CONV_WIDTH = 31
D_FF = 4 * D_MODEL
N_MOD = 6
EPS = 1e-6

LANES = 128
HEADS_PER_TILE = LANES // HEAD_DIM
CONV_HALO = 32
CONV_ROWS = 32
ROW_TILE = 512
KEY_GROUP = 4
SLOPE_PARTS = 3
MASKED = -1e30
VMEM_LIMIT = 56 * 1024 * 1024

F32 = jnp.float32
BF16 = jnp.bfloat16
_NT = (((1,), (1,)), ((), ()))


def _resident(shape, index_map):
    return pl.BlockSpec(shape, index_map, pipeline_mode=pl.Buffered(1))


def _params(*semantics):
    return pltpu.CompilerParams(dimension_semantics=semantics, vmem_limit_bytes=VMEM_LIMIT)


def _norm_mod(x, g, scale, shift):
    ms = jnp.mean(x * x, axis=-1, keepdims=True)
    return (x * lax.rsqrt(ms + EPS) * g) * (1.0 + scale) + shift


def _mod_kernel(c_ref, w_ref, b_ref, o_ref):
    c = c_ref[...]
    cond = c * jax.nn.sigmoid(c)
    o_ref[0, 0] = jnp.dot(cond, w_ref[0], precision=lax.Precision.HIGHEST,
                          preferred_element_type=F32) + b_ref[0]


def _modulation(c, ada_w, ada_b):
    depth = ada_w.shape[0]
    batch = c.shape[0]
    rows = 8 * pl.cdiv(batch, 8)
    c_pad = jnp.zeros((rows, D_MODEL), F32).at[:batch].set(c)
    out = pl.pallas_call(
        _mod_kernel,
        grid=(depth, N_MOD),
        in_specs=[
            pl.BlockSpec((rows, D_MODEL), lambda l, k: (0, 0)),
            pl.BlockSpec((1, D_MODEL, D_MODEL), lambda l, k: (l, 0, k)),
            pl.BlockSpec((1, 1, D_MODEL), lambda l, k: (l, 0, k)),
        ],
        out_specs=pl.BlockSpec((1, 1, rows, D_MODEL), lambda l, k: (l, k, 0, 0)),
        out_shape=jax.ShapeDtypeStruct((depth, N_MOD, rows, D_MODEL), F32),
        compiler_params=_params("arbitrary", "arbitrary"),
        name="adaln_modulation",
    )(c_pad, ada_w, ada_b.reshape(depth, 1, N_MOD * D_MODEL))
    mods = out[:, :, :batch].transpose(0, 2, 1, 3)
    return jnp.pad(mods, ((0, 0), (0, 0), (0, 8 - N_MOD), (0, 0)))


def _proj_kernel(x_ref, mod_ref, g_ref, w_ref, *rest, glu, has_bias):
    o_ref = rest[-1]
    m = mod_ref[0]
    h = _norm_mod(x_ref[...], g_ref[...], m[1:2], m[0:1])
    y = jnp.dot(h.astype(BF16), w_ref[...], preferred_element_type=F32)
    if has_bias:
        y = y + rest[0][...]
    if glu:
        half = y.shape[-1] // 2
        y = y[:, :half] * jax.nn.sigmoid(y[:, half:])
        for t in range(half // LANES):
            o_ref[t] = y[:, t * LANES:(t + 1) * LANES]
    else:
        o_ref[...] = y


def _projection(x, mod, g, w, b, seq, glu):
    tokens = x.shape[0]
    n_out = w.shape[1]
    n_res = n_out // 2 if glu else n_out
    tiles_per_seq = seq // ROW_TILE
    has_bias = b is not None
    in_specs = [
        pl.BlockSpec((ROW_TILE, D_MODEL), lambda t: (t, 0)),
        pl.BlockSpec((1, 8, D_MODEL), lambda t: (t // tiles_per_seq, 0, 0)),
        _resident((1, D_MODEL), lambda t: (0, 0)),
        _resident((D_MODEL, n_out), lambda t: (0, 0)),
    ]
    args = [x, mod, g.reshape(1, D_MODEL), w.astype(BF16)]
    if has_bias:
        in_specs.append(_resident((1, n_out), lambda t: (0, 0)))
        args.append(b.reshape(1, n_out))
    if glu:
        out_spec = pl.BlockSpec((n_res // LANES, ROW_TILE, LANES), lambda t: (0, t, 0))
        out_shape = jax.ShapeDtypeStruct((n_res // LANES, tokens, LANES), F32)
    else:
        out_spec = pl.BlockSpec((ROW_TILE, n_res), lambda t: (t, 0))
        out_shape = jax.ShapeDtypeStruct((tokens, n_res), F32)
    return pl.pallas_call(
        functools.partial(_proj_kernel, glu=glu, has_bias=has_bias),
        grid=(tokens // ROW_TILE,),
        in_specs=in_specs,
        out_specs=out_spec,
        out_shape=out_shape,
        compiler_params=_params("arbitrary"),
        name="glu_projection" if glu else "qkv_projection",
    )(*args)


def _conv_kernel(u_ref, dw_ref, dwb_ref, lng_ref, lnb_ref, o_ref, ext_ref, *, tiles_per_seq):
    t = pl.program_id(0)
    n_lt, rows, _ = u_ref.shape

    @pl.when(t % tiles_per_seq == 0)
    def _():
        ext_ref[:, 0:CONV_HALO, :] = jnp.zeros((n_lt, CONV_HALO, LANES), F32)

    @pl.when(t % tiles_per_seq != 0)
    def _():
        ext_ref[:, 0:CONV_HALO, :] = ext_ref[:, rows:rows + CONV_HALO, :]

    ext_ref[:, CONV_HALO:CONV_HALO + rows, :] = u_ref[...]

    first_tap = CONV_HALO - (CONV_WIDTH - 1)

    def chunk(c, carry):
        r0 = pl.multiple_of(c * CONV_ROWS, CONV_ROWS)
        accs = []
        for lt in range(n_lt):
            cols = slice(lt * LANES, (lt + 1) * LANES)
            acc = jnp.broadcast_to(dwb_ref[:, cols], (CONV_ROWS, LANES))
            for k in range(CONV_WIDTH):
                acc = acc + dw_ref[k:k + 1, cols] * ext_ref[lt, pl.ds(r0 + first_tap + k, CONV_ROWS), :]
            accs.append(acc)
        total = accs[0]
        for a in accs[1:]:
            total = total + a
        mu = jnp.sum(total, axis=-1, keepdims=True) * (1.0 / D_MODEL)
        cen = [a - mu for a in accs]
        sq = cen[0] * cen[0]
        for a in cen[1:]:
            sq = sq + a * a
        inv = lax.rsqrt(jnp.sum(sq, axis=-1, keepdims=True) * (1.0 / D_MODEL) + EPS)
        for lt in range(n_lt):
            cols = slice(lt * LANES, (lt + 1) * LANES)
            y = cen[lt] * inv * lng_ref[:, cols] + lnb_ref[:, cols]
            o_ref[pl.ds(r0, CONV_ROWS), cols] = (y * jax.nn.sigmoid(y)).astype(BF16)
        return carry

    lax.fori_loop(0, rows // CONV_ROWS, chunk, 0)


def _conv_mixer(u, dw, dw_b, ln_g, ln_b, seq):
    n_lt, tokens, _ = u.shape
    tiles_per_seq = seq // ROW_TILE
    taps = 8 * pl.cdiv(CONV_WIDTH, 8)
    dw_pad = jnp.zeros((taps, D_MODEL), F32).at[:CONV_WIDTH].set(dw)
    row = lambda v: v.reshape(1, D_MODEL)
    return pl.pallas_call(
        functools.partial(_conv_kernel, tiles_per_seq=tiles_per_seq),
        grid=(tokens // ROW_TILE,),
        in_specs=[
            pl.BlockSpec((n_lt, ROW_TILE, LANES), lambda t: (0, t, 0)),
            _resident((taps, D_MODEL), lambda t: (0, 0)),
            _resident((1, D_MODEL), lambda t: (0, 0)),
            _resident((1, D_MODEL), lambda t: (0, 0)),
            _resident((1, D_MODEL), lambda t: (0, 0)),
        ],
        out_specs=pl.BlockSpec((ROW_TILE, D_MODEL), lambda t: (t, 0)),
        out_shape=jax.ShapeDtypeStruct((tokens, D_MODEL), BF16),
        scratch_shapes=[pltpu.VMEM((n_lt, ROW_TILE + CONV_HALO, LANES), F32)],
        compiler_params=_params("arbitrary"),
        name="conv_mixer",
    )(u, dw_pad, row(dw_b), row(ln_g), row(ln_b))


def _attn_kernel(slope_ref, q_ref, k_ref, v_ref, gq_ref, gk_ref, o_ref,
                 ka_ref, kr_ref, vt_ref, km_ref, *, n_blocks):
    tile = pl.program_id(1)
    j = pl.program_id(2)
    blk = MOBA_BLOCK
    grp = KEY_GROUP * blk
    n_groups = ka_ref.shape[1] // grp

    lane = lax.broadcasted_iota(jnp.int32, (blk, LANES), 1)
    low = lane < HEAD_DIM
    lane_blk = lane & (HEAD_DIM - 1)

    def head_rms(x, g):
        x2 = x * x
        s0 = jnp.sum(jnp.where(low, x2, 0.0), axis=-1, keepdims=True)
        s1 = jnp.sum(jnp.where(low, 0.0, x2), axis=-1, keepdims=True)
        r = jnp.where(low, lax.rsqrt(s0 * (1.0 / HEAD_DIM) + EPS), lax.rsqrt(s1 * (1.0 / HEAD_DIM) + EPS))
        return x * r * g

    @pl.when(j == 0)
    def _():
        km_ref[...] = jnp.zeros(km_ref.shape, F32)
        pos = lax.broadcasted_iota(jnp.int32, (grp, LANES), 0) & (blk - 1)
        pos_lane = lax.broadcasted_iota(jnp.int32, (grp, LANES), 1) < SLOPE_PARTS
        kr_ref[...] = jnp.where(pos_lane, pos.astype(F32), 0.0).astype(BF16)

        def prep(g, carry):
            for i in range(KEY_GROUP):
                n = g * KEY_GROUP + i
                src = jnp.minimum(n, n_blocks - 1)
                r_src = pl.multiple_of(src * blk, blk)
                r_dst = pl.multiple_of(n * blk, blk)
                kn = head_rms(k_ref[0, pl.ds(r_src, blk), :], gk_ref[...])
                kn_bf = kn.astype(BF16)
                one_hot = jnp.where(lane_blk == n, 1.0, 0.0).astype(BF16)
                ka_ref[0, pl.ds(r_dst, blk), :] = jnp.where(low, kn_bf, one_hot)
                ka_ref[1, pl.ds(r_dst, blk), :] = jnp.where(low, one_hot, kn_bf)

                @pl.when(n < n_blocks)
                def _():
                    km_ref[pl.ds(n, 1), :] = jnp.mean(kn, axis=0, keepdims=True)

                vt_ref[g, :, i * blk:(i + 1) * blk] = v_ref[0, pl.ds(r_src, blk), :].T.astype(BF16)
            return carry

        lax.fori_loop(0, n_groups, prep, 0)

    q = head_rms(q_ref[0], gq_ref[...]) * (HEAD_DIM ** -0.5)
    q_bf = q.astype(BF16)
    km = km_ref[...]
    zero_bf = jnp.zeros((blk, LANES), BF16)

    blk_row = lax.broadcasted_iota(jnp.int32, (HEAD_DIM, blk), 0)
    blk_row_f = blk_row.astype(F32)
    key_i = lax.broadcasted_iota(jnp.int32, (blk, blk), 0)
    qry_i = lax.broadcasted_iota(jnp.int32, (blk, blk), 1)
    causal = key_i <= qry_i

    heads = []
    for h in range(HEADS_PER_TILE):
        mine = low if h == 0 else jnp.logical_not(low)
        slope = slope_ref[tile * HEADS_PER_TILE + h]

        gate = lax.dot_general(km, jnp.where(mine, q, 0.0), _NT,
                               precision=lax.Precision.HIGHEST, preferred_element_type=F32)
        gate = jnp.where(blk_row < j, gate, -jnp.inf)
        sel = jnp.zeros(gate.shape, F32)
        for _ in range(MOBA_TOP_K):
            best = jnp.max(gate, axis=0, keepdims=True)
            first = jnp.min(jnp.where(gate == best, blk_row_f, float(HEAD_DIM)), axis=0, keepdims=True)
            first = jnp.where(best > -jnp.inf, first, -1.0)
            pick = blk_row_f == first
            sel = jnp.where(pick, 1.0, sel)
            gate = jnp.where(pick, -jnp.inf, gate)
        mask_t = jnp.where(sel > 0.0, 0.0, MASKED)
        mask_q = jnp.concatenate([mask_t, mask_t], axis=0).T.astype(BF16)

        rest = jnp.full((blk, LANES), slope, F32)
        q_extra = jnp.zeros((blk, LANES), F32)
        for i in range(SLOPE_PARTS):
            piece = rest.astype(BF16).astype(F32)
            q_extra = jnp.where(lane == i, piece, q_extra)
            rest = rest - piece
        q_extra = q_extra.astype(BF16)

        q_past = jnp.concatenate([jnp.where(mine, q_bf, mask_q), q_extra], axis=1)
        q_own = jnp.concatenate([jnp.where(mine, q_bf, zero_bf), q_extra], axis=1)

        r_own = pl.multiple_of(j * blk, blk)
        k_own = jnp.concatenate([ka_ref[h, pl.ds(r_own, blk), :], kr_ref[0:blk, :]], axis=1)
        s = lax.dot_general(k_own, q_own, _NT, preferred_element_type=F32)
        s = jnp.where(causal, s, -jnp.inf)
        m = jnp.max(s, axis=0, keepdims=True)
        p = jnp.exp(s - m)
        l = jnp.sum(p, axis=0, keepdims=True)
        v_own = v_ref[0, pl.ds(r_own, blk), :].T.astype(BF16)
        acc = jnp.dot(v_own[h * HEAD_DIM:(h + 1) * HEAD_DIM], p.astype(BF16), preferred_element_type=F32)
        heads.append((slope, q_past, m, l, acc))

    def step(g, carry):
        out = []
        r0 = pl.multiple_of(g * grp, grp)
        for h in range(HEADS_PER_TILE):
            slope, q_past = heads[h][0], heads[h][1]
            m, l, acc = carry[h]
            k_aug = jnp.concatenate([ka_ref[h, pl.ds(r0, grp), :], kr_ref[...]], axis=1)
            s = lax.dot_general(k_aug, q_past, _NT, preferred_element_type=F32)
            offs = [(slope * float(blk)) * (g * KEY_GROUP + i - j).astype(F32) for i in range(KEY_GROUP)]
            tiles = [s[i * blk:(i + 1) * blk] for i in range(KEY_GROUP)]
            m_new = m
            for i in range(KEY_GROUP):
                m_new = jnp.maximum(m_new, jnp.max(tiles[i], axis=0, keepdims=True) + offs[i])
            alpha = jnp.exp(m - m_new)
            ps = [jnp.exp(tiles[i] - (m_new - offs[i])) for i in range(KEY_GROUP)]
            l_new = alpha * l
            for i in range(KEY_GROUP):
                l_new = l_new + jnp.sum(ps[i], axis=0, keepdims=True)
            p_all = jnp.concatenate([pi.astype(BF16) for pi in ps], axis=0)
            pv = jnp.dot(vt_ref[g, h * HEAD_DIM:(h + 1) * HEAD_DIM, :], p_all, preferred_element_type=F32)
            out.append((m_new, l_new, alpha * acc + pv))
        return tuple(out)

    init = tuple((hd[2], hd[3], hd[4]) for hd in heads)
    final = lax.fori_loop(0, (j + KEY_GROUP - 1) // KEY_GROUP, step, init)
    out_t = jnp.concatenate([acc / l for (_, l, acc) in final], axis=0)
    o_ref[0] = out_t.T.astype(BF16)


def _attention(qkv, q_g, k_g, batch, seq):
    assert seq % MOBA_BLOCK == 0 and HEADS_PER_TILE == 2
    n_blocks = seq // MOBA_BLOCK
    assert n_blocks <= HEAD_DIM, "selection columns hold one key block per lane of a head"
    n_groups = pl.cdiv(n_blocks, KEY_GROUP)
    assert n_groups * KEY_GROUP <= HEAD_DIM
    n_tiles = D_MODEL // LANES
    slopes = jnp.asarray(2.0 ** (-8.0 * np.arange(1, N_HEADS + 1) / N_HEADS), dtype=F32)
    qkv3 = qkv.reshape(batch, seq, 3 * D_MODEL)
    tile2 = lambda g: jnp.tile(g.reshape(1, HEAD_DIM), (1, HEADS_PER_TILE))
    out = pl.pallas_call(
        functools.partial(_attn_kernel, n_blocks=n_blocks),
        grid_spec=pltpu.PrefetchScalarGridSpec(
            num_scalar_prefetch=1,
            grid=(batch, n_tiles, n_blocks),
            in_specs=[
                pl.BlockSpec((1, MOBA_BLOCK, LANES), lambda b, t, j, s: (b, j, t)),
                pl.BlockSpec((1, seq, LANES), lambda b, t, j, s: (b, 0, n_tiles + t),
                             pipeline_mode=pl.Buffered(1)),
                pl.BlockSpec((1, seq, LANES), lambda b, t, j, s: (b, 0, 2 * n_tiles + t),
                             pipeline_mode=pl.Buffered(1)),
                pl.BlockSpec((1, LANES), lambda b, t, j, s: (0, 0)),
                pl.BlockSpec((1, LANES), lambda b, t, j, s: (0, 0)),
            ],
            out_specs=pl.BlockSpec((1, MOBA_BLOCK, LANES), lambda b, t, j, s: (b, j, t)),
            scratch_shapes=[
                pltpu.VMEM((HEADS_PER_TILE, n_groups * KEY_GROUP * MOBA_BLOCK, LANES), BF16),
                pltpu.VMEM((KEY_GROUP * MOBA_BLOCK, LANES), BF16),
                pltpu.VMEM((n_groups, LANES, KEY_GROUP * MOBA_BLOCK), BF16),
                pltpu.VMEM((HEAD_DIM, LANES), F32),
            ],
        ),
        out_shape=jax.ShapeDtypeStruct((batch, seq, D_MODEL), BF16),
        compiler_params=_params("arbitrary", "arbitrary", "arbitrary"),
        name="moba_attention",
    )(slopes, qkv3, qkv3, qkv3, tile2(q_g), tile2(k_g))
    return out.reshape(batch * seq, D_MODEL)


def _mlp_kernel(x_ref, y_ref, mod_ref, g_ref, wo_ref, bo_ref, w1_ref, w2_ref, o_ref):
    m = mod_ref[0]
    mixed = jnp.dot(y_ref[...], wo_ref[...], preferred_element_type=F32) + bo_ref[...]
    x = x_ref[...] + m[2:3] * mixed
    h = _norm_mod(x, g_ref[...], m[4:5], m[3:4]).astype(BF16)
    acc = jnp.zeros(x.shape, F32)
    for c in range(D_FF // D_MODEL):
        cols = slice(c * D_MODEL, (c + 1) * D_MODEL)
        a = jnp.maximum(jnp.dot(h, w1_ref[:, cols], preferred_element_type=F32), 0.0)
        acc = acc + jnp.dot((a * a).astype(BF16), w2_ref[cols, :], preferred_element_type=F32)
    o_ref[...] = x + m[5:6] * acc


def _mix_out_mlp(x, y, mod, g, w_o, b_o, w1, w2, seq):
    tokens = x.shape[0]
    tiles_per_seq = seq // ROW_TILE
    return pl.pallas_call(
        _mlp_kernel,
        grid=(tokens // ROW_TILE,),
        in_specs=[
            pl.BlockSpec((ROW_TILE, D_MODEL), lambda t: (t, 0)),
            pl.BlockSpec((ROW_TILE, D_MODEL), lambda t: (t, 0)),
            pl.BlockSpec((1, 8, D_MODEL), lambda t: (t // tiles_per_seq, 0, 0)),
            _resident((1, D_MODEL), lambda t: (0, 0)),
            _resident((D_MODEL, D_MODEL), lambda t: (0, 0)),
            _resident((1, D_MODEL), lambda t: (0, 0)),
            _resident((D_MODEL, D_FF), lambda t: (0, 0)),
            _resident((D_FF, D_MODEL), lambda t: (0, 0)),
        ],
        out_specs=pl.BlockSpec((ROW_TILE, D_MODEL), lambda t: (t, 0)),
        out_shape=jax.ShapeDtypeStruct((tokens, D_MODEL), F32),
        compiler_params=_params("arbitrary"),
        name="mix_out_mlp",
    )(x, y, mod, g.reshape(1, D_MODEL), w_o.astype(BF16), b_o.reshape(1, D_MODEL),
      w1.astype(BF16), w2.astype(BF16))


def kernel(x, c, ada_w, ada_b, mix_norm_g, mlp_norm_g, conv_w_in, conv_b_in, conv_dw, conv_dw_b, conv_ln_g, conv_ln_b, conv_w_out, conv_b_out, attn_w_qkv, attn_q_norm_g, attn_k_norm_g, attn_w_o, mlp_w1, mlp_w2):
    batch, seq, d = x.shape
    assert d == D_MODEL and seq % ROW_TILE == 0
    depth = ada_w.shape[0]
    mods = _modulation(c, ada_w, ada_b)
    xf = x.reshape(batch * seq, D_MODEL)
    for i in range(depth):
        j = i // 2
        if i % 2 == 0:
            u = _projection(xf, mods[i], mix_norm_g[i], conv_w_in[j], conv_b_in[j], seq, glu=True)
            y = _conv_mixer(u, conv_dw[j], conv_dw_b[j], conv_ln_g[j], conv_ln_b[j], seq)
            w_o, b_o = conv_w_out[j], conv_b_out[j]
        else:
            qkv = _projection(xf, mods[i], mix_norm_g[i], attn_w_qkv[j], None, seq, glu=False)
            y = _attention(qkv, attn_q_norm_g[j], attn_k_norm_g[j], batch, seq)
            w_o, b_o = attn_w_o[j], jnp.zeros((D_MODEL,), F32)
        xf = _mix_out_mlp(xf, y, mods[i], mlp_norm_g[i], w_o, b_o, mlp_w1[i], mlp_w2[i], seq)
    return xf.reshape(batch, seq, D_MODEL)
```

```python
import functools

import numpy as np
import jax
import jax.numpy as jnp
from jax import lax
from jax.experimental import pallas as pl
from jax.experimental.pallas import tpu as pltpu

D_MODEL = 1024
N_HEADS = 16
HEAD_DIM = D_MODEL // N_HEADS
MOBA_BLOCK = 256
MOBA_TOP_K = 3
CONV_WIDTH = 31
D_FF = 4 * D_MODEL
N_MOD = 6
EPS = 1e-6

LANES = 128
HEADS_PER_TILE = LANES // HEAD_DIM
CONV_HALO = 32
CONV_ROWS = 32
ROW_TILE = 512
KEY_GROUP = 4
SLOPE_PARTS = 3
MASKED = -1e30
LOG2E = 1.4426950408889634
V_ROWS = HEAD_DIM + 16
VMEM_LIMIT = 56 * 1024 * 1024

F32 = jnp.float32
BF16 = jnp.bfloat16
_NT = (((1,), (1,)), ((), ()))


def _resident(shape, index_map):
    return pl.BlockSpec(shape, index_map, pipeline_mode=pl.Buffered(1))


def _params(*semantics):
    return pltpu.CompilerParams(dimension_semantics=semantics, vmem_limit_bytes=VMEM_LIMIT)


def _norm_mod(x, g, scale, shift):
    ms = jnp.mean(x * x, axis=-1, keepdims=True)
    return (x * lax.rsqrt(ms + EPS) * g) * (1.0 + scale) + shift


def _mod_kernel(c_ref, w_ref, b_ref, o_ref):
    c = c_ref[...]
    cond = c * jax.nn.sigmoid(c)
    o_ref[0, 0] = jnp.dot(cond, w_ref[0], precision=lax.Precision.HIGHEST,
                          preferred_element_type=F32) + b_ref[0]


def _modulation(c, ada_w, ada_b):
    depth = ada_w.shape[0]
    batch = c.shape[0]
    rows = 8 * pl.cdiv(batch, 8)
    c_pad = jnp.zeros((rows, D_MODEL), F32).at[:batch].set(c)
    out = pl.pallas_call(
        _mod_kernel,
        grid=(depth, N_MOD),
        in_specs=[
            pl.BlockSpec((rows, D_MODEL), lambda l, k: (0, 0)),
            pl.BlockSpec((1, D_MODEL, D_MODEL), lambda l, k: (l, 0, k)),
            pl.BlockSpec((1, 1, D_MODEL), lambda l, k: (l, 0, k)),
        ],
        out_specs=pl.BlockSpec((1, 1, rows, D_MODEL), lambda l, k: (l, k, 0, 0)),
        out_shape=jax.ShapeDtypeStruct((depth, N_MOD, rows, D_MODEL), F32),
        compiler_params=_params("arbitrary", "arbitrary"),
        name="adaln_modulation",
    )(c_pad, ada_w, ada_b.reshape(depth, 1, N_MOD * D_MODEL))
    mods = out[:, :, :batch].transpose(0, 2, 1, 3)
    return jnp.pad(mods, ((0, 0), (0, 0), (0, 8 - N_MOD), (0, 0)))


def _proj_kernel(x_ref, mod_ref, g_ref, w_ref, *rest, glu, has_bias):
    o_ref = rest[-1]
    m = mod_ref[0]
    h = _norm_mod(x_ref[...], g_ref[...], m[1:2], m[0:1])
    y = jnp.dot(h.astype(BF16), w_ref[...], preferred_element_type=F32)
    if has_bias:
        y = y + rest[0][...]
    if glu:
        half = y.shape[-1] // 2
        y = y[:, :half] * jax.nn.sigmoid(y[:, half:])
        for t in range(half // LANES):
            o_ref[t] = y[:, t * LANES:(t + 1) * LANES]
    else:
        o_ref[...] = y


def _projection(x, mod, g, w, b, seq, glu):
    tokens = x.shape[0]
    n_out = w.shape[1]
    n_res = n_out // 2 if glu else n_out
    tiles_per_seq = seq // ROW_TILE
    has_bias = b is not None
    in_specs = [
        pl.BlockSpec((ROW_TILE, D_MODEL), lambda t: (t, 0)),
        pl.BlockSpec((1, 8, D_MODEL), lambda t: (t // tiles_per_seq, 0, 0)),
        _resident((1, D_MODEL), lambda t: (0, 0)),
        _resident((D_MODEL, n_out), lambda t: (0, 0)),
    ]
    args = [x, mod, g.reshape(1, D_MODEL), w.astype(BF16)]
    if has_bias:
        in_specs.append(_resident((1, n_out), lambda t: (0, 0)))
        args.append(b.reshape(1, n_out))
    if glu:
        out_spec = pl.BlockSpec((n_res // LANES, ROW_TILE, LANES), lambda t: (0, t, 0))
        out_shape = jax.ShapeDtypeStruct((n_res // LANES, tokens, LANES), F32)
    else:
        out_spec = pl.BlockSpec((ROW_TILE, n_res), lambda t: (t, 0))
        out_shape = jax.ShapeDtypeStruct((tokens, n_res), F32)
    return pl.pallas_call(
        functools.partial(_proj_kernel, glu=glu, has_bias=has_bias),
        grid=(tokens // ROW_TILE,),
        in_specs=in_specs,
        out_specs=out_spec,
        out_shape=out_shape,
        compiler_params=_params("arbitrary"),
        name="glu_projection" if glu else "qkv_projection",
    )(*args)


def _conv_kernel(u_ref, dw_ref, dwb_ref, lng_ref, lnb_ref, o_ref, ext_ref, *, tiles_per_seq):
    t = pl.program_id(0)
    n_lt, rows, _ = u_ref.shape

    @pl.when(t % tiles_per_seq == 0)
    def _():
        ext_ref[:, 0:CONV_HALO, :] = jnp.zeros((n_lt, CONV_HALO, LANES), F32)

    @pl.when(t % tiles_per_seq != 0)
    def _():
        ext_ref[:, 0:CONV_HALO, :] = ext_ref[:, rows:rows + CONV_HALO, :]

    ext_ref[:, CONV_HALO:CONV_HALO + rows, :] = u_ref[...]

    first_tap = CONV_HALO - (CONV_WIDTH - 1)

    def chunk(c, carry):
        r0 = pl.multiple_of(c * CONV_ROWS, CONV_ROWS)
        accs = []
        for lt in range(n_lt):
            cols = slice(lt * LANES, (lt + 1) * LANES)
            acc = jnp.broadcast_to(dwb_ref[:, cols], (CONV_ROWS, LANES))
            for k in range(CONV_WIDTH):
                acc = acc + dw_ref[k:k + 1, cols] * ext_ref[lt, pl.ds(r0 + first_tap + k, CONV_ROWS), :]
            accs.append(acc)
        total = accs[0]
        for a in accs[1:]:
            total = total + a
        mu = jnp.sum(total, axis=-1, keepdims=True) * (1.0 / D_MODEL)
        cen = [a - mu for a in accs]
        sq = cen[0] * cen[0]
        for a in cen[1:]:
            sq = sq + a * a
        inv = lax.rsqrt(jnp.sum(sq, axis=-1, keepdims=True) * (1.0 / D_MODEL) + EPS)
        for lt in range(n_lt):
            cols = slice(lt * LANES, (lt + 1) * LANES)
            y = cen[lt] * inv * lng_ref[:, cols] + lnb_ref[:, cols]
            o_ref[pl.ds(r0, CONV_ROWS), cols] = (y * jax.nn.sigmoid(y)).astype(BF16)
        return carry

    lax.fori_loop(0, rows // CONV_ROWS, chunk, 0)


def _conv_mixer(u, dw, dw_b, ln_g, ln_b, seq):
    n_lt, tokens, _ = u.shape
    tiles_per_seq = seq // ROW_TILE
    taps = 8 * pl.cdiv(CONV_WIDTH, 8)
    dw_pad = jnp.zeros((taps, D_MODEL), F32).at[:CONV_WIDTH].set(dw)
    row = lambda v: v.reshape(1, D_MODEL)
    return pl.pallas_call(
        functools.partial(_conv_kernel, tiles_per_seq=tiles_per_seq),
        grid=(tokens // ROW_TILE,),
        in_specs=[
            pl.BlockSpec((n_lt, ROW_TILE, LANES), lambda t: (0, t, 0)),
            _resident((taps, D_MODEL), lambda t: (0, 0)),
            _resident((1, D_MODEL), lambda t: (0, 0)),
            _resident((1, D_MODEL), lambda t: (0, 0)),
            _resident((1, D_MODEL), lambda t: (0, 0)),
        ],
        out_specs=pl.BlockSpec((ROW_TILE, D_MODEL), lambda t: (t, 0)),
        out_shape=jax.ShapeDtypeStruct((tokens, D_MODEL), BF16),
        scratch_shapes=[pltpu.VMEM((n_lt, ROW_TILE + CONV_HALO, LANES), F32)],
        compiler_params=_params("arbitrary"),
        name="conv_mixer",
    )(u, dw_pad, row(dw_b), row(ln_g), row(ln_b))


def _attn_kernel(slope_ref, q_ref, k_ref, v_ref, gq_ref, gk_ref, o_ref,
                 ka_ref, kr_ref, vt_ref, km_ref, sa_ref, sb_ref, pa_ref, pb_ref, *, n_blocks):
    tile = pl.program_id(1)
    j = pl.program_id(2)
    blk = MOBA_BLOCK
    grp = KEY_GROUP * blk
    n_groups = ka_ref.shape[1] // grp

    lane = lax.broadcasted_iota(jnp.int32, (blk, LANES), 1)
    low = lane < HEAD_DIM
    lane_blk = lane & (HEAD_DIM - 1)
    ones_rows = jnp.where(lax.broadcasted_iota(jnp.int32, (V_ROWS - HEAD_DIM, blk), 0) == 0, 1.0, 0.0)

    def head_rms(x, g):
        x2 = x * x
        s0 = jnp.sum(jnp.where(low, x2, 0.0), axis=-1, keepdims=True)
        s1 = jnp.sum(jnp.where(low, 0.0, x2), axis=-1, keepdims=True)
        r = jnp.where(low, lax.rsqrt(s0 * (1.0 / HEAD_DIM) + EPS), lax.rsqrt(s1 * (1.0 / HEAD_DIM) + EPS))
        return x * r * g

    @pl.when(j == 0)
    def _():
        km_ref[...] = jnp.zeros(km_ref.shape, F32)
        pos = lax.broadcasted_iota(jnp.int32, (grp, LANES), 0) & (blk - 1)
        pos_lane = lax.broadcasted_iota(jnp.int32, (grp, LANES), 1) < SLOPE_PARTS
        kr_ref[...] = jnp.where(pos_lane, pos.astype(F32), 0.0).astype(BF16)

        def prep(g, carry):
            for i in range(KEY_GROUP):
                n = g * KEY_GROUP + i
                src = jnp.minimum(n, n_blocks - 1)
                r_src = pl.multiple_of(src * blk, blk)
                r_dst = pl.multiple_of(n * blk, blk)
                kn = head_rms(k_ref[0, pl.ds(r_src, blk), :], gk_ref[...])
                kn_bf = kn.astype(BF16)
                one_hot = jnp.where(lane_blk == n, 1.0, 0.0).astype(BF16)
                ka_ref[0, pl.ds(r_dst, blk), :] = jnp.where(low, kn_bf, one_hot)
                ka_ref[1, pl.ds(r_dst, blk), :] = jnp.where(low, one_hot, kn_bf)

                @pl.when(n < n_blocks)
                def _():
                    km_ref[pl.ds(n, 1), :] = jnp.mean(kn, axis=0, keepdims=True)

                v_t = v_ref[0, pl.ds(r_src, blk), :].T
                for h in range(HEADS_PER_TILE):
                    vt_ref[g, h, :, i * blk:(i + 1) * blk] = jnp.concatenate(
                        [v_t[h * HEAD_DIM:(h + 1) * HEAD_DIM], ones_rows], axis=0).astype(BF16)
            return carry

        lax.fori_loop(0, n_groups, prep, 0)

    qn = head_rms(q_ref[0], gq_ref[...])
    q_bf = (qn * (HEAD_DIM ** -0.5 * LOG2E)).astype(BF16)

    km = km_ref[...]
    low_km = lax.broadcasted_iota(jnp.int32, km.shape, 1) < HEAD_DIM
    km_rows = jnp.concatenate([jnp.where(low_km, 0.0, km), jnp.where(low_km, km, 0.0)], axis=0)
    gate = lax.dot_general(km_rows, qn, _NT, precision=lax.Precision.HIGHEST, preferred_element_type=F32)
    blk_row = lax.broadcasted_iota(jnp.int32, (HEAD_DIM, blk), 0)
    blk_row_f = blk_row.astype(F32)
    masks = []
    for half in range(HEADS_PER_TILE):
        g_h = jnp.where(blk_row < j, gate[half * HEAD_DIM:(half + 1) * HEAD_DIM], -jnp.inf)
        sel = jnp.zeros(g_h.shape, F32)
        for _ in range(MOBA_TOP_K):
            best = jnp.max(g_h, axis=0, keepdims=True)
            first = jnp.min(jnp.where(g_h == best, blk_row_f, float(HEAD_DIM)), axis=0, keepdims=True)
            first = jnp.where(best > -jnp.inf, first, -1.0)
            pick = blk_row_f == first
            sel = jnp.where(pick, 1.0, sel)
            g_h = jnp.where(pick, -jnp.inf, g_h)
        masks.append(jnp.where(sel > 0.0, 0.0, MASKED))
    mask_q = jnp.concatenate(masks, axis=0).T.astype(BF16)

    key_i = lax.broadcasted_iota(jnp.int32, (blk, blk), 0)
    qry_i = lax.broadcasted_iota(jnp.int32, (blk, blk), 1)
    causal = key_i <= qry_i
    zero_bf = jnp.zeros((blk, LANES), BF16)
    r_own = pl.multiple_of(j * blk, blk)
    own_tile = pl.multiple_of((j & (KEY_GROUP - 1)) * blk, blk)
    pb_ref[...] = jnp.zeros(pb_ref.shape, BF16)

    slopes, q_past, ms = [], [], []
    for h in range(HEADS_PER_TILE):
        mine = low if h == 0 else jnp.logical_not(low)
        slope = slope_ref[tile * HEADS_PER_TILE + h] * LOG2E

        rest = jnp.full((blk, LANES), slope, F32)
        q_extra = jnp.zeros((blk, LANES), F32)
        for i in range(SLOPE_PARTS):
            piece = rest.astype(BF16).astype(F32)
            q_extra = jnp.where(lane == i, piece, q_extra)
            rest = rest - piece
        q_extra = q_extra.astype(BF16)

        q_own = jnp.concatenate([jnp.where(mine, q_bf, zero_bf), q_extra], axis=1)
        k_own = jnp.concatenate([ka_ref[h, pl.ds(r_own, blk), :], kr_ref[0:blk, :]], axis=1)
        s = lax.dot_general(k_own, q_own, _NT, preferred_element_type=F32)
        s = jnp.where(causal, s, -jnp.inf)
        m = jnp.max(s, axis=0, keepdims=True)
        pb_ref[h, pl.ds(own_tile, blk), :] = jnp.exp2(s - m).astype(BF16)

        slopes.append(slope)
        q_past.append(jnp.concatenate([jnp.where(mine, q_bf, mask_q), q_extra], axis=1))
        ms.append(m)

    def score_group(g, dst_ref):
        r0 = pl.multiple_of(g * grp, grp)
        for h in range(HEADS_PER_TILE):
            k_aug = jnp.concatenate([ka_ref[h, pl.ds(r0, grp), :], kr_ref[...]], axis=1)
            dst_ref[h] = lax.dot_general(k_aug, q_past[h], _NT, preferred_element_type=F32)

    def exp_group(g, s_ref, p_ref, ms):
        new_ms, scales = [], []
        for h in range(HEADS_PER_TILE):
            m, per_block = ms[h], []
            for i in range(KEY_GROUP):
                off = (slopes[h] * float(blk)) * (g * KEY_GROUP + i - j).astype(F32)
                tile_s = s_ref[h, i * blk:(i + 1) * blk, :]
                m_new = jnp.maximum(m, jnp.max(tile_s, axis=0, keepdims=True) + off)
                per_block.append(jnp.exp2(m - m_new))
                p_ref[h, i * blk:(i + 1) * blk, :] = jnp.exp2(tile_s - (m_new - off)).astype(BF16)
                m = m_new
            new_ms.append(m)
            scales.append(tuple(per_block))
        return tuple(new_ms), tuple(scales)

    def pv_group(g, p_ref, scales, accs):
        out = []
        for h in range(HEADS_PER_TILE):
            acc = accs[h]
            for i in range(KEY_GROUP):
                pv = jnp.dot(vt_ref[g, h, :, i * blk:(i + 1) * blk], p_ref[h, i * blk:(i + 1) * blk, :],
                             preferred_element_type=F32)
                acc = scales[h][i] * acc + pv
            out.append(acc)
        return tuple(out)

    last_group = n_groups - 1

    def step(t, carry):
        ms, accs, pending, pending_group = carry
        g0 = 2 * t
        score_group(jnp.minimum(g0 + 1, last_group), sb_ref)
        accs = pv_group(pending_group, pb_ref, pending, accs)
        ms, scales_a = exp_group(g0, sa_ref, pa_ref, ms)
        score_group(jnp.minimum(g0 + 2, last_group), sa_ref)
        accs = pv_group(g0, pa_ref, scales_a, accs)
        ms, scales_b = exp_group(g0 + 1, sb_ref, pb_ref, ms)
        return ms, accs, scales_b, g0 + 1

    score_group(0, sa_ref)
    one = jnp.ones((1, blk), F32)
    init = (tuple(ms),
            tuple(jnp.zeros((V_ROWS, blk), F32) for _ in range(HEADS_PER_TILE)),
            tuple(tuple(one for _ in range(KEY_GROUP)) for _ in range(HEADS_PER_TILE)),
            lax.shift_right_logical(j, jnp.int32(KEY_GROUP.bit_length() - 1)))
    n_iter = (j + KEY_GROUP - 1) // KEY_GROUP
    _, accs, pending, pending_group = lax.fori_loop(0, (n_iter + 1) // 2, step, init)
    accs = pv_group(pending_group, pb_ref, pending, accs)
    out_t = jnp.concatenate([acc[:HEAD_DIM] / acc[HEAD_DIM:HEAD_DIM + 1] for acc in accs], axis=0)
    o_ref[0] = out_t.T.astype(BF16)


def _attention(qkv, q_g, k_g, batch, seq):
    assert seq % MOBA_BLOCK == 0 and HEADS_PER_TILE == 2 and KEY_GROUP & (KEY_GROUP - 1) == 0
    n_blocks = seq // MOBA_BLOCK
    assert n_blocks <= HEAD_DIM, "selection columns hold one key block per lane of a head"
    n_iter_max = pl.cdiv(n_blocks - 1, KEY_GROUP)
    n_groups = max(pl.cdiv(n_blocks, KEY_GROUP), 2 * pl.cdiv(n_iter_max, 2), 1)
    assert n_groups * KEY_GROUP <= HEAD_DIM
    n_tiles = D_MODEL // LANES
    slopes = jnp.asarray(2.0 ** (-8.0 * np.arange(1, N_HEADS + 1) / N_HEADS), dtype=F32)
    qkv3 = qkv.reshape(batch, seq, 3 * D_MODEL)
    tile2 = lambda g: jnp.tile(g.reshape(1, HEAD_DIM), (1, HEADS_PER_TILE))
    group_rows = KEY_GROUP * MOBA_BLOCK
    out = pl.pallas_call(
        functools.partial(_attn_kernel, n_blocks=n_blocks),
        grid_spec=pltpu.PrefetchScalarGridSpec(
            num_scalar_prefetch=1,
            grid=(batch, n_tiles, n_blocks),
            in_specs=[
                pl.BlockSpec((1, MOBA_BLOCK, LANES), lambda b, t, j, s: (b, j, t)),
                pl.BlockSpec((1, seq, LANES), lambda b, t, j, s: (b, 0, n_tiles + t),
                             pipeline_mode=pl.Buffered(1)),
                pl.BlockSpec((1, seq, LANES), lambda b, t, j, s: (b, 0, 2 * n_tiles + t),
                             pipeline_mode=pl.Buffered(1)),
                pl.BlockSpec((1, LANES), lambda b, t, j, s: (0, 0)),
                pl.BlockSpec((1, LANES), lambda b, t, j, s: (0, 0)),
            ],
            out_specs=pl.BlockSpec((1, MOBA_BLOCK, LANES), lambda b, t, j, s: (b, j, t)),
            scratch_shapes=[
                pltpu.VMEM((HEADS_PER_TILE, n_groups * group_rows, LANES), BF16),
                pltpu.VMEM((group_rows, LANES), BF16),
                pltpu.VMEM((n_groups, HEADS_PER_TILE, V_ROWS, group_rows), BF16),
                pltpu.VMEM((HEAD_DIM, LANES), F32),
                pltpu.VMEM((HEADS_PER_TILE, group_rows, MOBA_BLOCK), F32),
                pltpu.VMEM((HEADS_PER_TILE, group_rows, MOBA_BLOCK), F32),
                pltpu.VMEM((HEADS_PER_TILE, group_rows, MOBA_BLOCK), BF16),
                pltpu.VMEM((HEADS_PER_TILE, group_rows, MOBA_BLOCK), BF16),
            ],
        ),
        out_shape=jax.ShapeDtypeStruct((batch, seq, D_MODEL), BF16),
        compiler_params=_params("arbitrary", "arbitrary", "arbitrary"),
        name="moba_attention",
    )(slopes, qkv3, qkv3, qkv3, tile2(q_g), tile2(k_g))
    return out.reshape(batch * seq, D_MODEL)


def _mlp_kernel(x_ref, y_ref, mod_ref, g_ref, wo_ref, bo_ref, w1_ref, w2_ref, o_ref):
    m = mod_ref[0]
    mixed = jnp.dot(y_ref[...], wo_ref[...], preferred_element_type=F32) + bo_ref[...]
    x = x_ref[...] + m[2:3] * mixed
    h = _norm_mod(x, g_ref[...], m[4:5], m[3:4]).astype(BF16)
    acc = jnp.zeros(x.shape, F32)
    for c in range(D_FF // D_MODEL):
        cols = slice(c * D_MODEL, (c + 1) * D_MODEL)
        a = jnp.maximum(jnp.dot(h, w1_ref[:, cols], preferred_element_type=F32), 0.0)
        acc = acc + jnp.dot((a * a).astype(BF16), w2_ref[cols, :], preferred_element_type=F32)
    o_ref[...] = x + m[5:6] * acc


def _mix_out_mlp(x, y, mod, g, w_o, b_o, w1, w2, seq):
    tokens = x.shape[0]
    tiles_per_seq = seq // ROW_TILE
    return pl.pallas_call(
        _mlp_kernel,
        grid=(tokens // ROW_TILE,),
        in_specs=[
            pl.BlockSpec((ROW_TILE, D_MODEL), lambda t: (t, 0)),
            pl.BlockSpec((ROW_TILE, D_MODEL), lambda t: (t, 0)),
            pl.BlockSpec((1, 8, D_MODEL), lambda t: (t // tiles_per_seq, 0, 0)),
            _resident((1, D_MODEL), lambda t: (0, 0)),
            _resident((D_MODEL, D_MODEL), lambda t: (0, 0)),
            _resident((1, D_MODEL), lambda t: (0, 0)),
            _resident((D_MODEL, D_FF), lambda t: (0, 0)),
            _resident((D_FF, D_MODEL), lambda t: (0, 0)),
        ],
        out_specs=pl.BlockSpec((ROW_TILE, D_MODEL), lambda t: (t, 0)),
        out_shape=jax.ShapeDtypeStruct((tokens, D_MODEL), F32),
        compiler_params=_params("arbitrary"),
        name="mix_out_mlp",
    )(x, y, mod, g.reshape(1, D_MODEL), w_o.astype(BF16), b_o.reshape(1, D_MODEL),
      w1.astype(BF16), w2.astype(BF16))


def kernel(x, c, ada_w, ada_b, mix_norm_g, mlp_norm_g, conv_w_in, conv_b_in, conv_dw, conv_dw_b, conv_ln_g, conv_ln_b, conv_w_out, conv_b_out, attn_w_qkv, attn_q_norm_g, attn_k_norm_g, attn_w_o, mlp_w1, mlp_w2):
    batch, seq, d = x.shape
    assert d == D_MODEL and seq % ROW_TILE == 0
    depth = ada_w.shape[0]
    mods = _modulation(c, ada_w, ada_b)
    xf = x.reshape(batch * seq, D_MODEL)
    for i in range(depth):
        j = i // 2
        if i % 2 == 0:
            u = _projection(xf, mods[i], mix_norm_g[i], conv_w_in[j], conv_b_in[j], seq, glu=True)
            y = _conv_mixer(u, conv_dw[j], conv_dw_b[j], conv_ln_g[j], conv_ln_b[j], seq)
            w_o, b_o = conv_w_out[j], conv_b_out[j]
        else:
            qkv = _projection(xf, mods[i], mix_norm_g[i], attn_w_qkv[j], None, seq, glu=False)
            y = _attention(qkv, attn_q_norm_g[j], attn_k_norm_g[j], batch, seq)
            w_o, b_o = attn_w_o[j], jnp.zeros((D_MODEL,), F32)
        xf = _mix_out_mlp(xf, y, mods[i], mlp_norm_g[i], w_o, b_o, mlp_w1[i], mlp_w2[i], seq)
    return xf.reshape(batch, seq, D_MODEL)
```
